```python
import jax, jax.numpy as jnp
from jax import lax
import numpy as np

D_MODEL = 2048
BATCH = 2
SEQ = 8192
DEPTH = 2

GRID_W = 64
CTX_LEN = 256
ROPE_THETA = 10000.0
Q_BLOCK = 128
LN_EPS = 1e-6
RMS_EPS = 1e-6

A_HEADS = 8
A_Q_LORA = 768
A_KV_LORA = 512
A_NOPE = 128
A_ROPE = 64
A_V = 128
A_SCALE = (A_NOPE + A_ROPE) ** -0.5
B_HEADS = 8
B_KV_HEADS = 2
B_GROUP = B_HEADS // B_KV_HEADS
B_HEAD_DIM = 128
B_SCALE = B_HEAD_DIM ** -0.5
A_WIDTH = A_HEADS * A_V
B_WIDTH = B_HEADS * B_HEAD_DIM
ATTN_WIDTH = A_WIDTH + B_WIDTH
ATTN_IN = A_Q_LORA + A_KV_LORA + A_ROPE + B_WIDTH + 2 * B_KV_HEADS * B_HEAD_DIM + ATTN_WIDTH
C_EXPAND = 2
C_WIDTH = C_EXPAND * D_MODEL
C_GROUPS = 16
C_GROUP_DIM = C_WIDTH // C_GROUPS

N_EVEN = (DEPTH + 1) // 2
N_ODD = DEPTH // 2
DEEPNORM_ALPHA = (2.0 * DEPTH) ** 0.25
DEEPNORM_BETA = (8.0 * DEPTH) ** -0.25

kernel_name = "hybrid_mla_gqa_fnet_deepnorm_dit"


def layer_norm(x):
    xf = x.astype(jnp.float32)
    mu = jnp.mean(xf, axis=-1, keepdims=True)
    var = jnp.mean(jnp.square(xf - mu), axis=-1, keepdims=True)
    return ((xf - mu) * lax.rsqrt(var + LN_EPS)).astype(x.dtype)


def rms_norm(x, g):
    xf = x.astype(jnp.float32)
    y = xf * lax.rsqrt(jnp.mean(xf * xf, axis=-1, keepdims=True) + RMS_EPS)
    return (y * g.astype(jnp.float32)).astype(x.dtype)


def axial_rope_angles(rows, rot_dim):
    r, col = jnp.meshgrid(jnp.arange(rows), jnp.arange(GRID_W), indexing="ij")
    r = r.reshape(-1).astype(jnp.float32)
    col = col.reshape(-1).astype(jnp.float32)
    n_freq = rot_dim // 4
    inv = ROPE_THETA ** (-jnp.arange(n_freq, dtype=jnp.float32) / n_freq)
    return jnp.concatenate([r[:, None] * inv, col[:, None] * inv], axis=-1)


def apply_rope(x, ang):
    cos = jnp.cos(ang)[None, :, None, :]
    sin = jnp.sin(ang)[None, :, None, :]
    xf = x.astype(jnp.float32).reshape(*x.shape[:-1], -1, 2)
    x1, x2 = xf[..., 0], xf[..., 1]
    out = jnp.stack([x1 * cos - x2 * sin, x1 * sin + x2 * cos], axis=-1)
    return out.reshape(x.shape).astype(x.dtype)


def block_attention(q, k, v, scale):
    b, s, g, r, dk = q.shape
    nblk = s // Q_BLOCK
    qb = jnp.moveaxis(q.reshape(b, nblk, Q_BLOCK, g, r, dk), 1, 0)

    def one_block(qi):
        sc = jnp.einsum("bqgrd,bkgd->bgrqk", qi, k, preferred_element_type=jnp.float32) * scale
        p = jax.nn.softmax(sc, axis=-1).astype(v.dtype)
        return jnp.einsum("bgrqk,bkge->bqgre", p, v)

    out = lax.map(one_block, qb)
    return jnp.moveaxis(out, 0, 1).reshape(b, s, g, r, v.shape[-1])


def attn_project(h, w_in, wq_b, q_lora_g, kv_lora_g, wkv_b, qn_g, kn_g):
    b, l, _ = h.shape
    splits = np.cumsum([A_Q_LORA, A_KV_LORA, A_ROPE, B_WIDTH,
                        B_KV_HEADS * B_HEAD_DIM, B_KV_HEADS * B_HEAD_DIM]).tolist()
    z = h @ w_in
    c_q, c_kv, k_pe, q_b, k_b, v_b, gate = jnp.split(z, splits, axis=-1)
    q_a = (rms_norm(c_q, q_lora_g) @ wq_b).reshape(b, l, A_HEADS, A_NOPE + A_ROPE)
    kv_a = (rms_norm(c_kv, kv_lora_g) @ wkv_b).reshape(b, l, A_HEADS, A_NOPE + A_V)
    q_nope, q_pe = q_a[..., :A_NOPE], q_a[..., A_NOPE:]
    k_nope, v_a = kv_a[..., :A_NOPE], kv_a[..., A_NOPE:]
    k_pe = k_pe.reshape(b, l, 1, A_ROPE)
    q_b = rms_norm(q_b.reshape(b, l, B_HEADS, B_HEAD_DIM), qn_g)
    k_b = rms_norm(k_b.reshape(b, l, B_KV_HEADS, B_HEAD_DIM), kn_g)
    v_b = v_b.reshape(b, l, B_KV_HEADS, B_HEAD_DIM)
    return q_nope, q_pe, k_nope, k_pe, v_a, q_b, k_b, v_b, gate


def mla_qk(q_nope, q_pe, k_nope, k_pe):
    b, l, h, _ = q_nope.shape
    q = jnp.concatenate([q_nope, q_pe], axis=-1)[:, :, :, None, :]
    k = jnp.concatenate([k_nope, jnp.broadcast_to(k_pe, (b, l, h, A_ROPE))], axis=-1)
    return q, k


def gated_out(y_a, y_b, gate, w_out):
    b, l = y_a.shape[:2]
    y = jnp.concatenate([y_a.reshape(b, l, A_WIDTH), y_b.reshape(b, l, B_WIDTH)], axis=-1)
    return (y * jax.nn.silu(gate)) @ w_out


def attention_mixer(h_lat, h_ctx, ang_a, ang_b, w_in, wq_b, q_lora_g, kv_lora_g, wkv_b,
                    qn_g, kn_g, w_out, with_ctx_queries):
    b, s, _ = h_lat.shape
    lqn, lqp, lkn, lkp, lva, lqb, lkb, lvb, lgate = attn_project(
        h_lat, w_in, wq_b, q_lora_g, kv_lora_g, wkv_b, qn_g, kn_g)
    cqn, cqp, ckn, ckp, cva, cqb, ckb, cvb, cgate = attn_project(
        h_ctx, w_in, wq_b, q_lora_g, kv_lora_g, wkv_b, qn_g, kn_g)
    lqp, lkp = apply_rope(lqp, ang_a), apply_rope(lkp, ang_a)
    lqb, lkb = apply_rope(lqb, ang_b), apply_rope(lkb, ang_b)
    lq_a, lk_a = mla_qk(lqn, lqp, lkn, lkp)
    cq_a, ck_a = mla_qk(cqn, cqp, ckn, ckp)
    k_a_all = jnp.concatenate([ck_a, lk_a], axis=1)
    v_a_all = jnp.concatenate([cva, lva], axis=1)
    k_b_all = jnp.concatenate([ckb, lkb], axis=1)
    v_b_all = jnp.concatenate([cvb, lvb], axis=1)
    y_a = block_attention(lq_a, k_a_all, v_a_all, A_SCALE)
    y_b = block_attention(lqb.reshape(b, s, B_KV_HEADS, B_GROUP, B_HEAD_DIM), k_b_all, v_b_all, B_SCALE)
    out_lat = gated_out(y_a, y_b, lgate, w_out)
    out_ctx = None
    if with_ctx_queries:
        lc = h_ctx.shape[1]
        yc_a = block_attention(cq_a, ck_a, cva, A_SCALE)
        yc_b = block_attention(cqb.reshape(b, lc, B_KV_HEADS, B_GROUP, B_HEAD_DIM), ckb, cvb, B_SCALE)
        out_ctx = gated_out(yc_a, yc_b, cgate, w_out)
    return out_lat, out_ctx


def fourier_mixer(h, w_in, w_out):
    b, l, _ = h.shape
    u, gate = jnp.split(h @ w_in, 2, axis=-1)
    ug = u.reshape(b, l, C_GROUPS, C_GROUP_DIM).astype(jnp.float32)
    f = jnp.fft.fft2(ug, axes=(1, 3), norm="ortho").real.astype(h.dtype)
    y = f.reshape(b, l, C_WIDTH) * jax.nn.silu(gate)
    return y @ w_out


def modulation(cvec, ada_w, ada_b):
    m = jax.nn.silu(cvec) @ ada_w + ada_b
    return jnp.split(m, 3, axis=-1)


def modulate(x, shift, scale):
    return layer_norm(x) * (1.0 + scale) + shift


def post_norm(x, out, gate, g, bias):
    return layer_norm(DEEPNORM_ALPHA * x + gate * out) * g + bias


def setup_inputs(seed: int = 0) -> dict:
    key = jax.random.key(seed)
    ks = jax.random.split(key, 24)

    def nrm(k, shape, s):
        return jax.random.normal(k, shape, jnp.float32) * s

    def gain(k, shape):
        return 1.0 + 0.02 * jax.random.normal(k, shape, jnp.float32)

    return {
        "x": nrm(ks[0], (BATCH, SEQ, D_MODEL), 1.0),
        "c": nrm(ks[1], (BATCH, D_MODEL), 1.0),
        "ctx": nrm(ks[2], (BATCH, CTX_LEN, D_MODEL), 1.0),
        "c_ctx": nrm(ks[3], (D_MODEL,), 1.0),
        "ada_w": nrm(ks[4], (DEPTH, D_MODEL, 3 * D_MODEL), 0.5 * D_MODEL ** -0.5),
        "ada_b": nrm(ks[5], (DEPTH, 3 * D_MODEL), 0.01),
        "ln_g": gain(ks[6], (DEPTH, D_MODEL)),
        "ln_b": nrm(ks[7], (DEPTH, D_MODEL), 0.01),
        "w_in_attn": nrm(ks[8], (N_EVEN, D_MODEL, ATTN_IN), D_MODEL ** -0.5),
        "wq_b": nrm(ks[9], (N_EVEN, A_Q_LORA, A_HEADS * (A_NOPE + A_ROPE)), A_Q_LORA ** -0.5),
        "q_lora_norm": gain(ks[10], (N_EVEN, A_Q_LORA)),
        "kv_lora_norm": gain(ks[11], (N_EVEN, A_KV_LORA)),
        "wkv_b": nrm(ks[12], (N_EVEN, A_KV_LORA, A_HEADS * (A_NOPE + A_V)), A_KV_LORA ** -0.5),
        "q_norm_b": gain(ks[13], (N_EVEN, B_HEAD_DIM)),
        "k_norm_b": gain(ks[14], (N_EVEN, B_HEAD_DIM)),
        "w_out_attn": nrm(ks[15], (N_EVEN, ATTN_WIDTH, D_MODEL), DEEPNORM_BETA * ATTN_WIDTH ** -0.5),
        "w_in_fourier": nrm(ks[16], (N_ODD, D_MODEL, 2 * C_WIDTH), D_MODEL ** -0.5),
        "w_out_fourier": nrm(ks[17], (N_ODD, C_WIDTH, D_MODEL), DEEPNORM_BETA * C_WIDTH ** -0.5),
    }


def reference(x, c, ctx, c_ctx, ada_w, ada_b, ln_g, ln_b, w_in_attn, wq_b, q_lora_norm,
              kv_lora_norm, wkv_b, q_norm_b, k_norm_b, w_out_attn, w_in_fourier, w_out_fourier):
    rows = x.shape[1] // GRID_W
    ang_a = axial_rope_angles(rows, A_ROPE)
    ang_b = axial_rope_angles(rows, B_HEAD_DIM)
    xc = ctx
    for i in range(DEPTH):
        j = i // 2
        ctx_needed = i < DEPTH - 1
        sh_l, sc_l, g_l = [m[:, None, :] for m in modulation(c, ada_w[i], ada_b[i])]
        h_lat = modulate(x, sh_l, sc_l)
        o_ctx = None
        g_c = None
        if i % 2 == 0:
            sh_c, sc_c, g_c = modulation(c_ctx, ada_w[i], ada_b[i])
            h_ctx = modulate(xc, sh_c, sc_c)
            o_lat, o_ctx = attention_mixer(
                h_lat, h_ctx, ang_a, ang_b, w_in_attn[j], wq_b[j], q_lora_norm[j],
                kv_lora_norm[j], wkv_b[j], q_norm_b[j], k_norm_b[j], w_out_attn[j], ctx_needed)
        else:
            o_lat = fourier_mixer(h_lat, w_in_fourier[j], w_out_fourier[j])
            if ctx_needed:
                sh_c, sc_c, g_c = modulation(c_ctx, ada_w[i], ada_b[i])
                o_ctx = fourier_mixer(modulate(xc, sh_c, sc_c), w_in_fourier[j], w_out_fourier[j])
        x = post_norm(x, o_lat, g_l, ln_g[i], ln_b[i])
        if ctx_needed:
            xc = post_norm(xc, o_ctx, g_c, ln_g[i], ln_b[i])
    return x
```

```python
import functools
import math

import numpy as np
import jax
import jax.numpy as jnp
from jax import lax
from jax.experimental import pallas as pl
from jax.experimental.pallas import tpu as pltpu

F32 = jnp.float32
BF16 = jnp.bfloat16

D_MODEL = 2048
DEPTH = 2
GRID_W = 64
ROPE_THETA = 10000.0
LN_EPS = 1e-6
RMS_EPS = 1e-6

A_HEADS = 8
A_Q_LORA = 768
A_KV_LORA = 512
A_NOPE = 128
A_ROPE = 64
A_V = 128
A_SCALE = (A_NOPE + A_ROPE) ** -0.5
B_HEADS = 8
B_KV_HEADS = 2
B_GROUP = B_HEADS // B_KV_HEADS
B_HEAD_DIM = 128
B_SCALE = B_HEAD_DIM ** -0.5
A_WIDTH = A_HEADS * A_V
B_WIDTH = B_HEADS * B_HEAD_DIM
ATTN_WIDTH = A_WIDTH + B_WIDTH
B_KV_WIDTH = B_KV_HEADS * B_HEAD_DIM

C_EXPAND = 2
C_WIDTH = C_EXPAND * D_MODEL
C_GROUPS = 16
C_GROUP_DIM = C_WIDTH // C_GROUPS

DEEPNORM_ALPHA = (2.0 * DEPTH) ** 0.25

LANE = 128
A_QK = A_NOPE + LANE
VMEM_LIMIT = 56 * 1024 * 1024

OFF_CQ = 0
OFF_CKV = OFF_CQ + A_Q_LORA
OFF_KPE = OFF_CKV + A_KV_LORA
OFF_QB = OFF_KPE + LANE
OFF_KB = OFF_QB + B_WIDTH
OFF_VB = OFF_KB + B_KV_WIDTH
OFF_GATE = OFF_VB + B_KV_WIDTH
PROJ0_WIDTH = OFF_GATE + ATTN_WIDTH

FFT_N1 = 128
FFT_N2 = 64


def _silu(x):
    return x / (1.0 + jnp.exp(-x))


def _layer_norm(x):
    mu = jnp.mean(x, axis=-1, keepdims=True)
    xc = x - mu
    var = jnp.mean(xc * xc, axis=-1, keepdims=True)
    return xc * lax.rsqrt(var + LN_EPS)


def _rms(x):
    return x * lax.rsqrt(jnp.mean(x * x, axis=-1, keepdims=True) + RMS_EPS)


def _rope(x, cos, sin_signed):
    return x * cos + pltpu.roll(x, LANE // 2, 1) * sin_signed


def _params(*sem):
    return pltpu.CompilerParams(dimension_semantics=sem, vmem_limit_bytes=VMEM_LIMIT)


def _resident(shape):
    return pl.BlockSpec(shape, lambda *_: (0,) * len(shape), pipeline_mode=pl.Buffered(1))


def _mod_kernel(cv_ref, w_ref, b_ref, o_ref):
    s = _silu(cv_ref[...])
    o_ref[0] = jnp.dot(s, w_ref[0], preferred_element_type=F32,
                       precision=lax.Precision.HIGHEST) + b_ref[0]


def _modulation(cvecs, ada_w, ada_b):
    rows = cvecs.shape[0]
    tn = 768
    return pl.pallas_call(
        _mod_kernel,
        grid=(DEPTH, 3 * D_MODEL // tn),
        in_specs=[
            pl.BlockSpec((rows, D_MODEL), lambda i, j: (0, 0)),
            pl.BlockSpec((1, D_MODEL, tn), lambda i, j: (i, 0, j)),
            pl.BlockSpec((1, 1, tn), lambda i, j: (i, 0, j)),
        ],
        out_specs=pl.BlockSpec((1, rows, tn), lambda i, j: (i, 0, j)),
        out_shape=jax.ShapeDtypeStruct((DEPTH, rows, 3 * D_MODEL), F32),
        compiler_params=_params("arbitrary", "arbitrary"),
        name="modulation",
    )(cvecs, ada_w, ada_b.reshape(DEPTH, 1, 3 * D_MODEL))


def _proj0_kernel(x_ref, ctx_ref, sh_ref, sc_ref, ca_ref, sa_ref, cb_ref, sb_ref,
                  w_ref, gq_ref, gkv_ref, wq_ref, wkv_ref, qn_ref, kn_ref,
                  qa_ref, ka_ref, va_ref, qb_ref, kb_ref, vb_ref, sg_ref, *, n_lat):
    is_ctx = pl.program_id(1) >= n_lat
    xt = jnp.where(is_ctx, ctx_ref[0], x_ref[0])
    h = (_layer_norm(xt) * (1.0 + sc_ref[0, 0]) + sh_ref[0, 0]).astype(BF16)

    def proj(lo, hi):
        return jnp.dot(h, w_ref[:, lo:hi], preferred_element_type=F32)

    ca, sa, cb, sb = ca_ref[...], sa_ref[...], cb_ref[...], sb_ref[...]

    cq = (_rms(proj(OFF_CQ, OFF_CKV)) * gq_ref[...]).astype(BF16)
    qa = jnp.dot(cq, wq_ref[...], preferred_element_type=F32)
    for hd in range(A_HEADS):
        lo = hd * A_QK
        qa_ref[0, :, lo:lo + A_NOPE] = (qa[:, lo:lo + A_NOPE] * A_SCALE).astype(BF16)
        qa_ref[0, :, lo + A_NOPE:lo + A_QK] = (
            _rope(qa[:, lo + A_NOPE:lo + A_QK], ca, sa) * A_SCALE).astype(BF16)

    ckv = (_rms(proj(OFF_CKV, OFF_KPE)) * gkv_ref[...]).astype(BF16)
    kv = jnp.dot(ckv, wkv_ref[...], preferred_element_type=F32)
    kpe = _rope(proj(OFF_KPE, OFF_QB), ca, sa).astype(BF16)
    for hd in range(A_HEADS):
        lo = hd * A_QK
        ka_ref[0, :, lo:lo + A_NOPE] = kv[:, hd * A_NOPE:(hd + 1) * A_NOPE].astype(BF16)
        ka_ref[0, :, lo + A_NOPE:lo + A_QK] = kpe
    va_ref[0] = kv[:, A_HEADS * A_NOPE:].astype(BF16)

    qb = proj(OFF_QB, OFF_KB)
    for hd in range(B_HEADS):
        lo = hd * B_HEAD_DIM
        n = _rms(qb[:, lo:lo + B_HEAD_DIM]) * qn_ref[...]
        qb_ref[0, :, lo:lo + B_HEAD_DIM] = (_rope(n, cb, sb) * B_SCALE).astype(BF16)
    kb = proj(OFF_KB, OFF_VB)
    for hd in range(B_KV_HEADS):
        lo = hd * B_HEAD_DIM
        n = _rms(kb[:, lo:lo + B_HEAD_DIM]) * kn_ref[...]
        kb_ref[0, :, lo:lo + B_HEAD_DIM] = _rope(n, cb, sb).astype(BF16)
    vb_ref[0] = proj(OFF_VB, OFF_GATE).astype(BF16)

    sg_ref[0] = _silu(proj(OFF_GATE, PROJ0_WIDTH)).astype(BF16)


def _proj0(x, ctx, sh, sc, tabs, w, gq, gkv, wq, wkv, qn, kn):
    b, s, d = x.shape
    lc = ctx.shape[1]
    tm = lc
    assert s % tm == 0
    n_lat = s // tm
    lk = s + lc
    widths = (A_HEADS * A_QK, A_HEADS * A_QK, A_WIDTH, B_WIDTH, B_KV_WIDTH, B_KV_WIDTH, ATTN_WIDTH)

    def tok(width):
        return pl.BlockSpec((1, tm, width), lambda bi, i: (bi, i, 0))

    tab = pl.BlockSpec((tm, LANE), lambda bi, i: (i, 0))
    mod = pl.BlockSpec((1, 1, 1, d), lambda bi, i: (bi, jnp.where(i >= n_lat, 1, 0), 0, 0))
    return pl.pallas_call(
        functools.partial(_proj0_kernel, n_lat=n_lat),
        grid=(b, n_lat + 1),
        in_specs=[
            pl.BlockSpec((1, tm, d), lambda bi, i: (bi, jnp.minimum(i, n_lat - 1), 0)),
            pl.BlockSpec((1, tm, d), lambda bi, i: (bi, 0, 0)),
            mod, mod, tab, tab, tab, tab,
            _resident(w.shape), _resident(gq.shape), _resident(gkv.shape),
            _resident(wq.shape), _resident(wkv.shape), _resident(qn.shape), _resident(kn.shape),
        ],
        out_specs=[tok(wd) for wd in widths],
        out_shape=[jax.ShapeDtypeStruct((b, lk, wd), BF16) for wd in widths],
        compiler_params=_params("arbitrary", "arbitrary"),
        name="proj0",
    )(x, ctx, sh, sc, *tabs, w, gq, gkv, wq, wkv, qn, kn)


def _flash_kernel(q_ref, k_ref, v_ref, o_ref, m_ref, l_ref, acc_ref, *, n_rep, dk, tk, chunks):
    tq = q_ref.shape[1]
    dv = v_ref.shape[2]
    q = q_ref[0]
    if n_rep > 1:
        q = jnp.concatenate([q[:, r * dk:(r + 1) * dk] for r in range(n_rep)], axis=0)

    m_ref[...] = jnp.full(m_ref.shape, -jnp.inf, F32)
    l_ref[...] = jnp.zeros(l_ref.shape, F32)
    acc_ref[...] = jnp.zeros(acc_ref.shape, F32)

    def step(k0, size):
        k = k_ref[0, pl.ds(k0, size), :]
        v = v_ref[0, pl.ds(k0, size), :]
        s = lax.dot_general(q, k, (((1,), (1,)), ((), ())), preferred_element_type=F32)
        m_prev = m_ref[...]
        m_new = jnp.maximum(m_prev, jnp.max(s, axis=-1, keepdims=True))
        p = jnp.exp(s - m_new)
        alpha = jnp.exp(m_prev - m_new)
        l_ref[...] = alpha * l_ref[...] + jnp.sum(p, axis=-1, keepdims=True)
        acc_ref[...] = alpha * acc_ref[...] + jnp.dot(p.astype(BF16), v, preferred_element_type=F32)
        m_ref[...] = m_new

    n_main, tail = chunks
    if n_main > 0:
        def body(c, carry):
            step(pl.multiple_of(c * tk, tk), tk)
            return carry
        lax.fori_loop(0, n_main, body, 0)
    if tail > 0:
        step(n_main * tk, tail)

    out = acc_ref[...] / l_ref[...]
    for r in range(n_rep):
        o_ref[0, :, r * dv:(r + 1) * dv] = out[r * tq:(r + 1) * tq].astype(o_ref.dtype)


def _attention(q, k, v, *, n_kv, n_rep, dk, dv, n_q, q_off, tq, tk, name):
    b, lk, _ = k.shape
    assert n_q % tq == 0 and q_off % tq == 0
    tk = min(tk, lk)
    chunks = (lk // tk, lk % tk)
    rows = n_rep * tq
    return pl.pallas_call(
        functools.partial(_flash_kernel, n_rep=n_rep, dk=dk, tk=tk, chunks=chunks),
        grid=(b, n_kv, n_q // tq),
        in_specs=[
            pl.BlockSpec((1, tq, n_rep * dk), lambda bi, g, i: (bi, i + q_off // tq, g)),
            pl.BlockSpec((1, lk, dk), lambda bi, g, i: (bi, 0, g)),
            pl.BlockSpec((1, lk, dv), lambda bi, g, i: (bi, 0, g)),
        ],
        out_specs=pl.BlockSpec((1, tq, n_rep * dv), lambda bi, g, i: (bi, i, g)),
        out_shape=jax.ShapeDtypeStruct((b, n_q, n_kv * n_rep * dv), BF16),
        scratch_shapes=[
            pltpu.VMEM((rows, 1), F32),
            pltpu.VMEM((rows, 1), F32),
            pltpu.VMEM((rows, dv), F32),
        ],
        compiler_params=_params("arbitrary", "arbitrary", "arbitrary"),
        name=name,
    )(q, k, v)


def _post_norm(x, out, gate, g, bias):
    return _layer_norm(DEEPNORM_ALPHA * x + gate * out) * g + bias


def _out0_kernel(ya_ref, yb_ref, sg_ref, x_ref, w_ref, gate_ref, g_ref, b_ref, sh_ref, sc_ref,
                 x1_ref, h1_ref):
    sg = sg_ref[0].astype(F32)
    ga = (ya_ref[0].astype(F32) * sg[:, :A_WIDTH]).astype(BF16)
    gb = (yb_ref[0].astype(F32) * sg[:, A_WIDTH:]).astype(BF16)
    out = (jnp.dot(ga, w_ref[:A_WIDTH, :], preferred_element_type=F32)
           + jnp.dot(gb, w_ref[A_WIDTH:, :], preferred_element_type=F32))
    x1 = _post_norm(x_ref[0], out, gate_ref[0], g_ref[...], b_ref[...])
    x1_ref[0] = x1
    h1_ref[0] = (_layer_norm(x1) * (1.0 + sc_ref[0]) + sh_ref[0]).astype(BF16)


def _out0(ya, yb, sg, sg_off, x, w, gate, g, bias, sh, sc, *, tm):
    b, s, d = x.shape
    assert s % tm == 0 and sg_off % tm == 0

    def tok(width, off=0):
        return pl.BlockSpec((1, tm, width), lambda bi, i: (bi, i + off // tm, 0))

    vec = pl.BlockSpec((1, 1, d), lambda bi, i: (bi, 0, 0))
    return pl.pallas_call(
        _out0_kernel,
        grid=(b, s // tm),
        in_specs=[tok(A_WIDTH), tok(B_WIDTH), tok(ATTN_WIDTH, sg_off), tok(d),
                  _resident(w.shape), vec, _resident(g.shape), _resident(bias.shape), vec, vec],
        out_specs=[tok(d), tok(d)],
        out_shape=[jax.ShapeDtypeStruct((b, s, d), F32), jax.ShapeDtypeStruct((b, s, d), BF16)],
        compiler_params=_params("arbitrary", "arbitrary"),
        name="out0",
    )(ya, yb, sg, x, w, gate, g, bias, sh, sc)


def _matmul_kernel(x_ref, w_ref, o_ref, *, gate):
    z = jnp.dot(x_ref[...], w_ref[...], preferred_element_type=F32)
    o_ref[...] = (_silu(z) if gate else z).astype(o_ref.dtype)


def _matmul(x, w, *, gate, tm, tn, name):
    m, kdim = x.shape
    n = w.shape[1]
    assert m % tm == 0 and n % tn == 0
    return pl.pallas_call(
        functools.partial(_matmul_kernel, gate=gate),
        grid=(m // tm, n // tn),
        in_specs=[pl.BlockSpec((tm, kdim), lambda i, j: (i, 0)),
                  pl.BlockSpec((kdim, tn), lambda i, j: (0, j))],
        out_specs=pl.BlockSpec((tm, tn), lambda i, j: (i, j)),
        out_shape=jax.ShapeDtypeStruct((m, n), BF16),
        compiler_params=_params("arbitrary", "arbitrary"),
        name=name,
    )(x, w)


def _dft1_kernel(x_ref, d_ref, o_ref):
    y = jnp.dot(d_ref[...], x_ref[0], preferred_element_type=F32)
    o_ref[0, 0] = y[:FFT_N2].astype(BF16)
    o_ref[0, 1] = y[FFT_N2:].astype(BF16)


def _dft2_kernel(y_ref, m_ref, o_ref):
    y = jnp.concatenate([y_ref[0, 0, 0], y_ref[0, 1, 0]], axis=0)
    z = jnp.dot(m_ref[0], y, preferred_element_type=F32)
    o_ref[0, 0] = z[:FFT_N1].astype(BF16)
    o_ref[0, 1] = z[FFT_N1:].astype(BF16)


def _position_dft(u, d1, m2):
    b, l, c = u.shape
    assert l == FFT_N1 * FFT_N2
    tcol = 2 * c
    y = pl.pallas_call(
        _dft1_kernel,
        grid=(b, FFT_N1 * c // tcol),
        in_specs=[pl.BlockSpec((1, FFT_N2, tcol), lambda bi, j: (bi, 0, j)),
                  _resident(d1.shape)],
        out_specs=pl.BlockSpec((1, 2, FFT_N2, tcol), lambda bi, j: (bi, 0, 0, j)),
        out_shape=jax.ShapeDtypeStruct((b, 2, FFT_N2, FFT_N1 * c), BF16),
        compiler_params=_params("arbitrary", "arbitrary"),
        name="dft_stage1",
    )(u.reshape(b, FFT_N2, FFT_N1 * c), d1)
    z = pl.pallas_call(
        _dft2_kernel,
        grid=(b, FFT_N2),
        in_specs=[pl.BlockSpec((1, 2, 1, FFT_N1, c), lambda bi, k2: (bi, 0, k2, 0, 0)),
                  pl.BlockSpec((1, 2 * FFT_N1, 2 * FFT_N1), lambda bi, k2: (k2, 0, 0))],
        out_specs=pl.BlockSpec((1, 2, FFT_N1, c), lambda bi, k2: (bi, 0, 0, k2)),
        out_shape=jax.ShapeDtypeStruct((b, 2, FFT_N1, FFT_N2 * c), BF16),
        compiler_params=_params("arbitrary", "arbitrary"),
        name="dft_stage2",
    )(y.reshape(b, 2, FFT_N2, FFT_N1, c), m2)
    return z.reshape(b, 2, l, c)


def _out1_kernel(zr_ref, zi_ref, cc_ref, sc_ref, sg_ref, x_ref, w_ref, gate_ref, g_ref, b_ref,
                 o_ref, y_ref):
    for grp in range(C_GROUPS):
        lo = grp * C_GROUP_DIM
        f = (jnp.dot(zr_ref[0, 0, :, lo:lo + C_GROUP_DIM], cc_ref[...], preferred_element_type=F32)
             + jnp.dot(zi_ref[0, 0, :, lo:lo + C_GROUP_DIM], sc_ref[...], preferred_element_type=F32))
        y_ref[:, lo:lo + C_GROUP_DIM] = (
            f * sg_ref[0, :, lo:lo + C_GROUP_DIM].astype(F32)).astype(BF16)
    out = jnp.dot(y_ref[...], w_ref[...], preferred_element_type=F32)
    o_ref[0] = _post_norm(x_ref[0], out, gate_ref[0], g_ref[...], b_ref[...])


def _out1(z, cc, sc, sg, x, w, gate, g, bias, *, tm):
    b, s, d = x.shape
    c = z.shape[-1]
    assert s % tm == 0

    def tok(width):
        return pl.BlockSpec((1, tm, width), lambda bi, i: (bi, i, 0))

    vec = pl.BlockSpec((1, 1, d), lambda bi, i: (bi, 0, 0))
    return pl.pallas_call(
        _out1_kernel,
        grid=(b, s // tm),
        in_specs=[pl.BlockSpec((1, 1, tm, c), lambda bi, i: (bi, 0, i, 0)),
                  pl.BlockSpec((1, 1, tm, c), lambda bi, i: (bi, 1, i, 0)),
                  _resident(cc.shape), _resident(sc.shape), tok(c), tok(d),
                  _resident(w.shape), vec, _resident(g.shape), _resident(bias.shape)],
        out_specs=tok(d),
        out_shape=jax.ShapeDtypeStruct((b, s, d), F32),
        scratch_shapes=[pltpu.VMEM((tm, c), BF16)],
        compiler_params=_params("arbitrary", "arbitrary"),
        name="out1",
    )(z, z, cc, sc, sg, x, w, gate, g, bias)


def _split_pairs(w, heads, dim):
    lead = w.shape[:-1]
    w = w.reshape(*lead, heads, dim // 2, 2)
    return jnp.swapaxes(w, -1, -2).reshape(*lead, heads * dim)


def _pe_block(w):
    z = jnp.zeros(w.shape[:-1] + (LANE // 4,), w.dtype)
    return jnp.concatenate([w[..., 0::2], z, w[..., 1::2], z], axis=-1)


def _layout_w_in(w):
    offs = np.cumsum([0, A_Q_LORA, A_KV_LORA, A_ROPE, B_WIDTH, B_KV_WIDTH, B_KV_WIDTH, ATTN_WIDTH])
    cq, ckv, kpe, qb, kb, vb, gate = [w[:, offs[i]:offs[i + 1]] for i in range(7)]
    return jnp.concatenate(
        [cq, ckv, _pe_block(kpe), _split_pairs(qb, B_HEADS, B_HEAD_DIM),
         _split_pairs(kb, B_KV_HEADS, B_HEAD_DIM), vb, gate], axis=1).astype(BF16)


def _layout_wq_b(w):
    w = w.reshape(A_Q_LORA, A_HEADS, A_NOPE + A_ROPE)
    w = jnp.concatenate([w[..., :A_NOPE], _pe_block(w[..., A_NOPE:])], axis=-1)
    return w.reshape(A_Q_LORA, A_HEADS * A_QK).astype(BF16)


def _layout_wkv_b(w):
    w = w.reshape(A_KV_LORA, A_HEADS, 2, A_NOPE)
    return jnp.swapaxes(w, 1, 2).reshape(A_KV_LORA, 2 * A_HEADS * A_NOPE).astype(BF16)


def _rope_tables(rows, n_ctx):
    r = jnp.repeat(jnp.arange(rows, dtype=F32), GRID_W)
    col = jnp.tile(jnp.arange(GRID_W, dtype=F32), rows)

    def angles(rot_dim):
        n_freq = rot_dim // 4
        inv = ROPE_THETA ** (-jnp.arange(n_freq, dtype=F32) / n_freq)
        ang = jnp.concatenate([r[:, None] * inv, col[:, None] * inv], axis=-1)
        return jnp.concatenate([ang, jnp.zeros((n_ctx, rot_dim // 2), F32)], axis=0)

    ang_a = angles(A_ROPE)
    ang_b = angles(B_HEAD_DIM)
    za = jnp.zeros_like(ang_a)
    ca = jnp.concatenate([jnp.cos(ang_a), za, jnp.cos(ang_a), za], axis=-1)
    sa = jnp.concatenate([-jnp.sin(ang_a), za, jnp.sin(ang_a), za], axis=-1)
    cb = jnp.concatenate([jnp.cos(ang_b), jnp.cos(ang_b)], axis=-1)
    sb = jnp.concatenate([-jnp.sin(ang_b), jnp.sin(ang_b)], axis=-1)
    return ca, sa, cb, sb


def _dft_matrices():
    l = FFT_N1 * FFT_N2
    t1 = 2.0 * math.pi * ((jnp.arange(FFT_N2)[:, None] * jnp.arange(FFT_N2)[None, :]) % FFT_N2) / FFT_N2
    d1 = jnp.concatenate([jnp.cos(t1), -jnp.sin(t1)], axis=0) / math.sqrt(FFT_N2)
    k2 = jnp.arange(FFT_N2)[:, None, None]
    k1 = jnp.arange(FFT_N1)[None, :, None]
    n1 = jnp.arange(FFT_N1)[None, None, :]
    t2 = 2.0 * math.pi * ((k1 * n1 * FFT_N2 + n1 * k2) % l).astype(F32) / l
    mr = jnp.cos(t2) / math.sqrt(FFT_N1)
    mi = -jnp.sin(t2) / math.sqrt(FFT_N1)
    m2 = jnp.concatenate([jnp.concatenate([mr, -mi], axis=2),
                          jnp.concatenate([mi, mr], axis=2)], axis=1)
    t3 = 2.0 * math.pi * ((jnp.arange(C_GROUP_DIM)[:, None] * jnp.arange(C_GROUP_DIM)[None, :])
                          % C_GROUP_DIM) / C_GROUP_DIM
    cc = jnp.cos(t3) / math.sqrt(C_GROUP_DIM)
    sc = jnp.sin(t3) / math.sqrt(C_GROUP_DIM)
    return d1.astype(BF16), m2.astype(BF16), cc.astype(BF16), sc.astype(BF16)


def kernel(x, c, ctx, c_ctx, ada_w, ada_b, ln_g, ln_b, w_in_attn, wq_b, q_lora_norm, kv_lora_norm,
           wkv_b, q_norm_b, k_norm_b, w_out_attn, w_in_fourier, w_out_fourier):
    b, s, d = x.shape
    lc = ctx.shape[1]
    assert d == D_MODEL and s % GRID_W == 0

    cvecs = jnp.zeros((8, d), F32).at[:b].set(c).at[b].set(c_ctx)
    mods = _modulation(cvecs, ada_w, ada_b)
    shift, scale, gate = mods[..., :d], mods[..., d:2 * d], mods[..., 2 * d:]

    def lat(m, i):
        return m[i, :b, None, :]

    def both(m, i):
        return jnp.stack([m[i, :b], jnp.broadcast_to(m[i, b], (b, d))], axis=1)[:, :, None, :]

    qa, ka, va, qb, kb, vb, sg = _proj0(
        x, ctx, both(shift, 0), both(scale, 0), _rope_tables(s // GRID_W, lc),
        _layout_w_in(w_in_attn[0]), q_lora_norm[0][None, :], kv_lora_norm[0][None, :],
        _layout_wq_b(wq_b[0]), _layout_wkv_b(wkv_b[0]),
        _split_pairs(q_norm_b[0], 1, B_HEAD_DIM)[None, :], _split_pairs(k_norm_b[0], 1, B_HEAD_DIM)[None, :])

    attn_a = functools.partial(_attention, n_kv=A_HEADS, n_rep=1, dk=A_QK, dv=A_V)
    attn_b = functools.partial(_attention, n_kv=B_KV_HEADS, n_rep=B_GROUP, dk=B_HEAD_DIM, dv=B_HEAD_DIM)
    ya = attn_a(qa, ka, va, n_q=s, q_off=0, tq=512, tk=512, name="attn_a")
    yb = attn_b(qb, kb, vb, n_q=s, q_off=0, tq=128, tk=512, name="attn_b")

    w_out0 = w_out_attn[0].astype(BF16)
    g0, b0 = ln_g[0][None, :], ln_b[0][None, :]
    x1, h1 = _out0(ya, yb, sg, 0, x, w_out0, lat(gate, 0), g0, b0, lat(shift, 1), lat(scale, 1), tm=512)

    ctx_rows = lambda t: t[:, s:]
    yca = attn_a(qa, ctx_rows(ka), ctx_rows(va), n_q=lc, q_off=s, tq=lc, tk=lc, name="attn_a_ctx")
    ycb = attn_b(qb, ctx_rows(kb), ctx_rows(vb), n_q=lc, q_off=s, tq=lc // 2, tk=lc, name="attn_b_ctx")
    g_ctx = jnp.broadcast_to(gate[0, b], (b, 1, d))
    zeros = jnp.zeros((b, 1, d), F32)
    xc1, _ = _out0(yca, ycb, sg, s, ctx, w_out0, g_ctx, g0, b0, zeros, zeros, tm=lc)
    del xc1

    h1 = h1.reshape(b * s, d)
    w1 = w_in_fourier[0].astype(BF16)
    u = _matmul(h1, w1[:, :C_WIDTH], gate=False, tm=2048, tn=1024, name="proj1_u")
    sg1 = _matmul(h1, w1[:, C_WIDTH:], gate=True, tm=2048, tn=1024, name="proj1_gate")
    d1, m2, cc, sc = _dft_matrices()
    z = _position_dft(u.reshape(b, s, C_WIDTH), d1, m2)
    return _out1(z, cc, sc, sg1.reshape(b, s, C_WIDTH), x1, w_out_fourier[0].astype(BF16),
                 lat(gate, 1), ln_g[1][None, :], ln_b[1][None, :], tm=256)
```

```python
import functools
import math

import numpy as np
import jax
import jax.numpy as jnp
from jax import lax
from jax.experimental import pallas as pl
from jax.experimental.pallas import tpu as pltpu

F32 = jnp.float32
BF16 = jnp.bfloat16

D_MODEL = 2048
DEPTH = 2
GRID_W = 64
ROPE_THETA = 10000.0
LN_EPS = 1e-6
RMS_EPS = 1e-6

A_HEADS = 8
A_Q_LORA = 768
A_KV_LORA = 512
A_NOPE = 128
A_ROPE = 64
A_V = 128
A_SCALE = (A_NOPE + A_ROPE) ** -0.5
B_HEADS = 8
B_KV_HEADS = 2
B_GROUP = B_HEADS // B_KV_HEADS
B_HEAD_DIM = 128
B_SCALE = B_HEAD_DIM ** -0.5
A_WIDTH = A_HEADS * A_V
B_WIDTH = B_HEADS * B_HEAD_DIM
ATTN_WIDTH = A_WIDTH + B_WIDTH
B_KV_WIDTH = B_KV_HEADS * B_HEAD_DIM

C_EXPAND = 2
C_WIDTH = C_EXPAND * D_MODEL
C_GROUPS = 16
C_GROUP_DIM = C_WIDTH // C_GROUPS

DEEPNORM_ALPHA = (2.0 * DEPTH) ** 0.25

LANE = 128
A_QK = A_NOPE + LANE
VMEM_LIMIT = 56 * 1024 * 1024

OFF_CQ = 0
OFF_CKV = OFF_CQ + A_Q_LORA
OFF_KPE = OFF_CKV + A_KV_LORA
OFF_QB = OFF_KPE + LANE
OFF_KB = OFF_QB + B_WIDTH
OFF_GATE = OFF_KB + B_KV_WIDTH
PROJ0_WIDTH = OFF_GATE + ATTN_WIDTH

LOG2E = math.log2(math.e)
CONTRACT_LAST = (((1,), (1,)), ((), ()))

FFT_N1 = 128
FFT_N2 = 64


def _silu(x):
    return x / (1.0 + jnp.exp(-x))


def _layer_norm(x):
    mu = jnp.mean(x, axis=-1, keepdims=True)
    xc = x - mu
    var = jnp.mean(xc * xc, axis=-1, keepdims=True)
    return xc * lax.rsqrt(var + LN_EPS)


def _rms(x):
    return x * lax.rsqrt(jnp.mean(x * x, axis=-1, keepdims=True) + RMS_EPS)


def _rope(x, cos, sin_signed):
    return x * cos + pltpu.roll(x, LANE // 2, 1) * sin_signed


def _params(*sem):
    return pltpu.CompilerParams(dimension_semantics=sem, vmem_limit_bytes=VMEM_LIMIT)


def _resident(shape):
    return pl.BlockSpec(shape, lambda *_: (0,) * len(shape), pipeline_mode=pl.Buffered(1))


def _mod_kernel(cv_ref, w_ref, b_ref, o_ref):
    s = _silu(cv_ref[...])
    o_ref[0] = jnp.dot(s, w_ref[0], preferred_element_type=F32,
                       precision=lax.Precision.HIGHEST) + b_ref[0]


def _modulation(cvecs, ada_w, ada_b):
    rows = cvecs.shape[0]
    tn = 768
    return pl.pallas_call(
        _mod_kernel,
        grid=(DEPTH, 3 * D_MODEL // tn),
        in_specs=[
            pl.BlockSpec((rows, D_MODEL), lambda i, j: (0, 0)),
            pl.BlockSpec((1, D_MODEL, tn), lambda i, j: (i, 0, j)),
            pl.BlockSpec((1, 1, tn), lambda i, j: (i, 0, j)),
        ],
        out_specs=pl.BlockSpec((1, rows, tn), lambda i, j: (i, 0, j)),
        out_shape=jax.ShapeDtypeStruct((DEPTH, rows, 3 * D_MODEL), F32),
        compiler_params=_params("arbitrary", "arbitrary"),
        name="modulation",
    )(cvecs, ada_w, ada_b.reshape(DEPTH, 1, 3 * D_MODEL))


def _proj0_kernel(x_ref, ctx_ref, sh_ref, sc_ref, ca_ref, sa_ref, cb_ref, sb_ref,
                  w_ref, wvbt_ref, gq_ref, gkv_ref, wq_ref, wk_ref, wvt_ref, qn_ref, kn_ref,
                  qa_ref, ka_ref, vat_ref, qb_ref, kb_ref, vbt_ref, sg_ref, *, n_lat):
    is_ctx = pl.program_id(1) >= n_lat
    xt = jnp.where(is_ctx, ctx_ref[0], x_ref[0])
    h = (_layer_norm(xt) * (1.0 + sc_ref[0, 0]) + sh_ref[0, 0]).astype(BF16)

    def proj(lo, hi):
        return jnp.dot(h, w_ref[:, lo:hi], preferred_element_type=F32)

    ca, sa, cb, sb = ca_ref[...], sa_ref[...], cb_ref[...], sb_ref[...]

    cq = (_rms(proj(OFF_CQ, OFF_CKV)) * gq_ref[...]).astype(BF16)
    qa = jnp.dot(cq, wq_ref[...], preferred_element_type=F32)
    for hd in range(A_HEADS):
        lo = hd * A_QK
        qa_ref[0, :, lo:lo + A_NOPE] = (qa[:, lo:lo + A_NOPE] * (A_SCALE * LOG2E)).astype(BF16)
        qa_ref[0, :, lo + A_NOPE:lo + A_QK] = (
            _rope(qa[:, lo + A_NOPE:lo + A_QK], ca, sa) * (A_SCALE * LOG2E)).astype(BF16)

    ckv = (_rms(proj(OFF_CKV, OFF_KPE)) * gkv_ref[...]).astype(BF16)
    kn = jnp.dot(ckv, wk_ref[...], preferred_element_type=F32)
    kpe = _rope(proj(OFF_KPE, OFF_QB), ca, sa).astype(BF16)
    for hd in range(A_HEADS):
        lo = hd * A_QK
        ka_ref[0, :, lo:lo + A_NOPE] = kn[:, hd * A_NOPE:(hd + 1) * A_NOPE].astype(BF16)
        ka_ref[0, :, lo + A_NOPE:lo + A_QK] = kpe
    vat_ref[0] = lax.dot_general(wvt_ref[...], ckv, CONTRACT_LAST,
                                 preferred_element_type=F32).astype(BF16)

    qb = proj(OFF_QB, OFF_KB)
    for hd in range(B_HEADS):
        lo = hd * B_HEAD_DIM
        n = _rms(qb[:, lo:lo + B_HEAD_DIM]) * qn_ref[...]
        qb_ref[0, :, lo:lo + B_HEAD_DIM] = (_rope(n, cb, sb) * (B_SCALE * LOG2E)).astype(BF16)
    kb = proj(OFF_KB, OFF_GATE)
    for hd in range(B_KV_HEADS):
        lo = hd * B_HEAD_DIM
        n = _rms(kb[:, lo:lo + B_HEAD_DIM]) * kn_ref[...]
        kb_ref[0, :, lo:lo + B_HEAD_DIM] = _rope(n, cb, sb).astype(BF16)
    vbt_ref[0] = lax.dot_general(wvbt_ref[...], h, CONTRACT_LAST,
                                 preferred_element_type=F32).astype(BF16)

    sg_ref[0] = _silu(proj(OFF_GATE, PROJ0_WIDTH)).astype(BF16)


def _proj0(x, ctx, sh, sc, tabs, w, wvbt, gq, gkv, wq, wk, wvt, qn, kn):
    b, s, d = x.shape
    lc = ctx.shape[1]
    tm = lc
    assert s % tm == 0
    n_lat = s // tm
    lk = s + lc

    def tok(width):
        return (pl.BlockSpec((1, tm, width), lambda bi, i: (bi, i, 0)),
                jax.ShapeDtypeStruct((b, lk, width), BF16))

    def chan(width):
        return (pl.BlockSpec((1, width, tm), lambda bi, i: (bi, 0, i)),
                jax.ShapeDtypeStruct((b, width, lk), BF16))

    outs = [tok(A_HEADS * A_QK), tok(A_HEADS * A_QK), chan(A_WIDTH), tok(B_WIDTH),
            tok(B_KV_WIDTH), chan(B_KV_WIDTH), tok(ATTN_WIDTH)]
    tab = pl.BlockSpec((tm, LANE), lambda bi, i: (i, 0))
    mod = pl.BlockSpec((1, 1, 1, d), lambda bi, i: (bi, jnp.where(i >= n_lat, 1, 0), 0, 0))
    weights = (w, wvbt, gq, gkv, wq, wk, wvt, qn, kn)
    return pl.pallas_call(
        functools.partial(_proj0_kernel, n_lat=n_lat),
        grid=(b, n_lat + 1),
        in_specs=[
            pl.BlockSpec((1, tm, d), lambda bi, i: (bi, jnp.minimum(i, n_lat - 1), 0)),
            pl.BlockSpec((1, tm, d), lambda bi, i: (bi, 0, 0)),
            mod, mod, tab, tab, tab, tab,
        ] + [_resident(a.shape) for a in weights],
        out_specs=[o[0] for o in outs],
        out_shape=[o[1] for o in outs],
        compiler_params=_params("arbitrary", "arbitrary"),
        name="proj0",
    )(x, ctx, sh, sc, *tabs, *weights)


def _flash_kernel(q_ref, k_ref, vt_ref, o_ref, acc_ref, *, n_rep, dk, tk, unroll):
    tq = q_ref.shape[1]
    dv = vt_ref.shape[1]
    lk = k_ref.shape[1]
    q = q_ref[0]
    if n_rep > 1:
        q = jnp.concatenate([q[:, r * dk:(r + 1) * dk] for r in range(n_rep)], axis=0)
    rows = n_rep * tq
    acc_ref[...] = jnp.zeros(acc_ref.shape, F32)

    def step(k0, size, m, l):
        k = k_ref[0, pl.ds(k0, size), :]
        vt = vt_ref[0, :, pl.ds(k0, size)]
        st = lax.dot_general(k, q, CONTRACT_LAST, preferred_element_type=F32)
        m_new = jnp.maximum(m, jnp.max(st, axis=0, keepdims=True))
        p = jnp.exp2(st - m_new)
        alpha = jnp.exp2(m - m_new)
        l = alpha * l + jnp.sum(p, axis=0, keepdims=True)
        acc_ref[...] = alpha * acc_ref[...] + jnp.dot(vt, p.astype(BF16), preferred_element_type=F32)
        return m_new, l

    m = jnp.full((1, rows), -jnp.inf, F32)
    l = jnp.zeros((1, rows), F32)
    n_groups = lk // (tk * unroll)
    if n_groups > 0:
        def body(g, carry):
            m, l = carry
            for u in range(unroll):
                m, l = step(pl.multiple_of((g * unroll + u) * tk, tk), tk, m, l)
            return m, l
        m, l = lax.fori_loop(0, n_groups, body, (m, l))
    k0 = n_groups * tk * unroll
    while k0 < lk:
        size = min(tk, lk - k0)
        m, l = step(k0, size, m, l)
        k0 += size

    out = (acc_ref[...] / l).T
    for r in range(n_rep):
        o_ref[0, :, r * dv:(r + 1) * dv] = out[r * tq:(r + 1) * tq].astype(o_ref.dtype)


def _attention(q, k, vt, *, n_kv, n_rep, dk, dv, n_q, q_off, tq, tk, unroll, name):
    b, lk, _ = k.shape
    assert n_q % tq == 0 and q_off % tq == 0
    tk = min(tk, lk)
    return pl.pallas_call(
        functools.partial(_flash_kernel, n_rep=n_rep, dk=dk, tk=tk, unroll=unroll),
        grid=(b, n_kv, n_q // tq),
        in_specs=[
            pl.BlockSpec((1, tq, n_rep * dk), lambda bi, g, i: (bi, i + q_off // tq, g)),
            pl.BlockSpec((1, lk, dk), lambda bi, g, i: (bi, 0, g)),
            pl.BlockSpec((1, dv, lk), lambda bi, g, i: (bi, g, 0)),
        ],
        out_specs=pl.BlockSpec((1, tq, n_rep * dv), lambda bi, g, i: (bi, i, g)),
        out_shape=jax.ShapeDtypeStruct((b, n_q, n_kv * n_rep * dv), BF16),
        scratch_shapes=[pltpu.VMEM((dv, n_rep * tq), F32)],
        compiler_params=_params("arbitrary", "arbitrary", "arbitrary"),
        name=name,
    )(q, k, vt)


def _post_norm(x, out, gate, g, bias):
    return _layer_norm(DEEPNORM_ALPHA * x + gate * out) * g + bias


def _out0_kernel(ya_ref, yb_ref, sg_ref, x_ref, w_ref, gate_ref, g_ref, b_ref, sh_ref, sc_ref,
                 x1_ref, h1_ref):
    sg = sg_ref[0].astype(F32)
    ga = (ya_ref[0].astype(F32) * sg[:, :A_WIDTH]).astype(BF16)
    gb = (yb_ref[0].astype(F32) * sg[:, A_WIDTH:]).astype(BF16)
    out = (jnp.dot(ga, w_ref[:A_WIDTH, :], preferred_element_type=F32)
           + jnp.dot(gb, w_ref[A_WIDTH:, :], preferred_element_type=F32))
    x1 = _post_norm(x_ref[0], out, gate_ref[0], g_ref[...], b_ref[...])
    x1_ref[0] = x1
    h1_ref[0] = (_layer_norm(x1) * (1.0 + sc_ref[0]) + sh_ref[0]).astype(BF16)


def _out0(ya, yb, sg, sg_off, x, w, gate, g, bias, sh, sc, *, tm):
    b, s, d = x.shape
    assert s % tm == 0 and sg_off % tm == 0

    def tok(width, off=0):
        return pl.BlockSpec((1, tm, width), lambda bi, i: (bi, i + off // tm, 0))

    vec = pl.BlockSpec((1, 1, d), lambda bi, i: (bi, 0, 0))
    return pl.pallas_call(
        _out0_kernel,
        grid=(b, s // tm),
        in_specs=[tok(A_WIDTH), tok(B_WIDTH), tok(ATTN_WIDTH, sg_off), tok(d),
                  _resident(w.shape), vec, _resident(g.shape), _resident(bias.shape), vec, vec],
        out_specs=[tok(d), tok(d)],
        out_shape=[jax.ShapeDtypeStruct((b, s, d), F32), jax.ShapeDtypeStruct((b, s, d), BF16)],
        compiler_params=_params("arbitrary", "arbitrary"),
        name="out0",
    )(ya, yb, sg, x, w, gate, g, bias, sh, sc)


def _matmul_kernel(x_ref, w_ref, o_ref, *, gate):
    z = jnp.dot(x_ref[...], w_ref[...], preferred_element_type=F32)
    o_ref[...] = (_silu(z) if gate else z).astype(o_ref.dtype)


def _matmul(x, w, *, gate, tm, tn, name):
    m, kdim = x.shape
    n = w.shape[1]
    assert m % tm == 0 and n % tn == 0
    return pl.pallas_call(
        functools.partial(_matmul_kernel, gate=gate),
        grid=(m // tm, n // tn),
        in_specs=[pl.BlockSpec((tm, kdim), lambda i, j: (i, 0)),
                  pl.BlockSpec((kdim, tn), lambda i, j: (0, j))],
        out_specs=pl.BlockSpec((tm, tn), lambda i, j: (i, j)),
        out_shape=jax.ShapeDtypeStruct((m, n), BF16),
        compiler_params=_params("arbitrary", "arbitrary"),
        name=name,
    )(x, w)


def _dft1_kernel(x_ref, d_ref, o_ref):
    y = jnp.dot(d_ref[...], x_ref[0], preferred_element_type=F32)
    o_ref[0, 0] = y[:FFT_N2].astype(BF16)
    o_ref[0, 1] = y[FFT_N2:].astype(BF16)


def _dft2_kernel(y_ref, m_ref, o_ref):
    y = jnp.concatenate([y_ref[0, 0, 0], y_ref[0, 1, 0]], axis=0)
    z = jnp.dot(m_ref[0], y, preferred_element_type=F32)
    o_ref[0, 0] = z[:FFT_N1].astype(BF16)
    o_ref[0, 1] = z[FFT_N1:].astype(BF16)


def _position_dft(u, d1, m2):
    b, l, c = u.shape
    assert l == FFT_N1 * FFT_N2
    tcol = 2 * c
    y = pl.pallas_call(
        _dft1_kernel,
        grid=(b, FFT_N1 * c // tcol),
        in_specs=[pl.BlockSpec((1, FFT_N2, tcol), lambda bi, j: (bi, 0, j)),
                  _resident(d1.shape)],
        out_specs=pl.BlockSpec((1, 2, FFT_N2, tcol), lambda bi, j: (bi, 0, 0, j)),
        out_shape=jax.ShapeDtypeStruct((b, 2, FFT_N2, FFT_N1 * c), BF16),
        compiler_params=_params("arbitrary", "arbitrary"),
        name="dft_stage1",
    )(u.reshape(b, FFT_N2, FFT_N1 * c), d1)
    z = pl.pallas_call(
        _dft2_kernel,
        grid=(b, FFT_N2),
        in_specs=[pl.BlockSpec((1, 2, 1, FFT_N1, c), lambda bi, k2: (bi, 0, k2, 0, 0)),
                  pl.BlockSpec((1, 2 * FFT_N1, 2 * FFT_N1), lambda bi, k2: (k2, 0, 0))],
        out_specs=pl.BlockSpec((1, 2, FFT_N1, c), lambda bi, k2: (bi, 0, 0, k2)),
        out_shape=jax.ShapeDtypeStruct((b, 2, FFT_N1, FFT_N2 * c), BF16),
        compiler_params=_params("arbitrary", "arbitrary"),
        name="dft_stage2",
    )(y.reshape(b, 2, FFT_N2, FFT_N1, c), m2)
    return z.reshape(b, 2, l, c)


def _out1_kernel(zr_ref, zi_ref, cc_ref, sc_ref, sg_ref, x_ref, w_ref, gate_ref, g_ref, b_ref,
                 o_ref, y_ref):
    for grp in range(C_GROUPS):
        lo = grp * C_GROUP_DIM
        f = (jnp.dot(zr_ref[0, 0, :, lo:lo + C_GROUP_DIM], cc_ref[...], preferred_element_type=F32)
             + jnp.dot(zi_ref[0, 0, :, lo:lo + C_GROUP_DIM], sc_ref[...], preferred_element_type=F32))
        y_ref[:, lo:lo + C_GROUP_DIM] = (
            f * sg_ref[0, :, lo:lo + C_GROUP_DIM].astype(F32)).astype(BF16)
    out = jnp.dot(y_ref[...], w_ref[...], preferred_element_type=F32)
    o_ref[0] = _post_norm(x_ref[0], out, gate_ref[0], g_ref[...], b_ref[...])


def _out1(z, cc, sc, sg, x, w, gate, g, bias, *, tm):
    b, s, d = x.shape
    c = z.shape[-1]
    assert s % tm == 0

    def tok(width):
        return pl.BlockSpec((1, tm, width), lambda bi, i: (bi, i, 0))

    vec = pl.BlockSpec((1, 1, d), lambda bi, i: (bi, 0, 0))
    return pl.pallas_call(
        _out1_kernel,
        grid=(b, s // tm),
        in_specs=[pl.BlockSpec((1, 1, tm, c), lambda bi, i: (bi, 0, i, 0)),
                  pl.BlockSpec((1, 1, tm, c), lambda bi, i: (bi, 1, i, 0)),
                  _resident(cc.shape), _resident(sc.shape), tok(c), tok(d),
                  _resident(w.shape), vec, _resident(g.shape), _resident(bias.shape)],
        out_specs=tok(d),
        out_shape=jax.ShapeDtypeStruct((b, s, d), F32),
        scratch_shapes=[pltpu.VMEM((tm, c), BF16)],
        compiler_params=_params("arbitrary", "arbitrary"),
        name="out1",
    )(z, z, cc, sc, sg, x, w, gate, g, bias)


def _split_pairs(w, heads, dim):
    lead = w.shape[:-1]
    w = w.reshape(*lead, heads, dim // 2, 2)
    return jnp.swapaxes(w, -1, -2).reshape(*lead, heads * dim)


def _pe_block(w):
    z = jnp.zeros(w.shape[:-1] + (LANE // 4,), w.dtype)
    return jnp.concatenate([w[..., 0::2], z, w[..., 1::2], z], axis=-1)


def _layout_w_in(w):
    offs = np.cumsum([0, A_Q_LORA, A_KV_LORA, A_ROPE, B_WIDTH, B_KV_WIDTH, B_KV_WIDTH, ATTN_WIDTH])
    cq, ckv, kpe, qb, kb, vb, gate = [w[:, offs[i]:offs[i + 1]] for i in range(7)]
    cols = jnp.concatenate(
        [cq, ckv, _pe_block(kpe), _split_pairs(qb, B_HEADS, B_HEAD_DIM),
         _split_pairs(kb, B_KV_HEADS, B_HEAD_DIM), gate], axis=1)
    return cols.astype(BF16), vb.T.astype(BF16)


def _layout_wq_b(w):
    w = w.reshape(A_Q_LORA, A_HEADS, A_NOPE + A_ROPE)
    w = jnp.concatenate([w[..., :A_NOPE], _pe_block(w[..., A_NOPE:])], axis=-1)
    return w.reshape(A_Q_LORA, A_HEADS * A_QK).astype(BF16)


def _layout_wkv_b(w):
    w = w.reshape(A_KV_LORA, A_HEADS, A_NOPE + A_V)
    wk = w[..., :A_NOPE].reshape(A_KV_LORA, A_HEADS * A_NOPE)
    wv = w[..., A_NOPE:].reshape(A_KV_LORA, A_HEADS * A_V)
    return wk.astype(BF16), wv.T.astype(BF16)


def _rope_tables(rows, n_ctx):
    r = jnp.repeat(jnp.arange(rows, dtype=F32), GRID_W)
    col = jnp.tile(jnp.arange(GRID_W, dtype=F32), rows)

    def angles(rot_dim):
        n_freq = rot_dim // 4
        inv = ROPE_THETA ** (-jnp.arange(n_freq, dtype=F32) / n_freq)
        ang = jnp.concatenate([r[:, None] * inv, col[:, None] * inv], axis=-1)
        return jnp.concatenate([ang, jnp.zeros((n_ctx, rot_dim // 2), F32)], axis=0)

    ang_a = angles(A_ROPE)
    ang_b = angles(B_HEAD_DIM)
    za = jnp.zeros_like(ang_a)
    ca = jnp.concatenate([jnp.cos(ang_a), za, jnp.cos(ang_a), za], axis=-1)
    sa = jnp.concatenate([-jnp.sin(ang_a), za, jnp.sin(ang_a), za], axis=-1)
    cb = jnp.concatenate([jnp.cos(ang_b), jnp.cos(ang_b)], axis=-1)
    sb = jnp.concatenate([-jnp.sin(ang_b), jnp.sin(ang_b)], axis=-1)
    return ca, sa, cb, sb


def _dft_matrices():
    l = FFT_N1 * FFT_N2
    t1 = 2.0 * math.pi * ((jnp.arange(FFT_N2)[:, None] * jnp.arange(FFT_N2)[None, :]) % FFT_N2) / FFT_N2
    d1 = jnp.concatenate([jnp.cos(t1), -jnp.sin(t1)], axis=0) / math.sqrt(FFT_N2)
    k2 = jnp.arange(FFT_N2)[:, None, None]
    k1 = jnp.arange(FFT_N1)[None, :, None]
    n1 = jnp.arange(FFT_N1)[None, None, :]
    t2 = 2.0 * math.pi * ((k1 * n1 * FFT_N2 + n1 * k2) % l).astype(F32) / l
    mr = jnp.cos(t2) / math.sqrt(FFT_N1)
    mi = -jnp.sin(t2) / math.sqrt(FFT_N1)
    m2 = jnp.concatenate([jnp.concatenate([mr, -mi], axis=2),
                          jnp.concatenate([mi, mr], axis=2)], axis=1)
    t3 = 2.0 * math.pi * ((jnp.arange(C_GROUP_DIM)[:, None] * jnp.arange(C_GROUP_DIM)[None, :])
                          % C_GROUP_DIM) / C_GROUP_DIM
    cc = jnp.cos(t3) / math.sqrt(C_GROUP_DIM)
    sc = jnp.sin(t3) / math.sqrt(C_GROUP_DIM)
    return d1.astype(BF16), m2.astype(BF16), cc.astype(BF16), sc.astype(BF16)


def kernel(x, c, ctx, c_ctx, ada_w, ada_b, ln_g, ln_b, w_in_attn, wq_b, q_lora_norm, kv_lora_norm,
           wkv_b, q_norm_b, k_norm_b, w_out_attn, w_in_fourier, w_out_fourier):
    b, s, d = x.shape
    lc = ctx.shape[1]
    assert d == D_MODEL and s % GRID_W == 0

    cvecs = jnp.zeros((8, d), F32).at[:b].set(c).at[b].set(c_ctx)
    mods = _modulation(cvecs, ada_w, ada_b)
    shift, scale, gate = mods[..., :d], mods[..., d:2 * d], mods[..., 2 * d:]

    def lat(m, i):
        return m[i, :b, None, :]

    def both(m, i):
        return jnp.stack([m[i, :b], jnp.broadcast_to(m[i, b], (b, d))], axis=1)[:, :, None, :]

    w0, wvbt = _layout_w_in(w_in_attn[0])
    wk, wvt = _layout_wkv_b(wkv_b[0])
    qa, ka, vat, qb, kb, vbt, sg = _proj0(
        x, ctx, both(shift, 0), both(scale, 0), _rope_tables(s // GRID_W, lc),
        w0, wvbt, q_lora_norm[0][None, :], kv_lora_norm[0][None, :], _layout_wq_b(wq_b[0]), wk, wvt,
        _split_pairs(q_norm_b[0], 1, B_HEAD_DIM)[None, :], _split_pairs(k_norm_b[0], 1, B_HEAD_DIM)[None, :])

    attn_a = functools.partial(_attention, n_kv=A_HEADS, n_rep=1, dk=A_QK, dv=A_V)
    attn_b = functools.partial(_attention, n_kv=B_KV_HEADS, n_rep=B_GROUP, dk=B_HEAD_DIM, dv=B_HEAD_DIM)
    ya = attn_a(qa, ka, vat, n_q=s, q_off=0, tq=512, tk=512, unroll=2, name="attn_a")
    yb = attn_b(qb, kb, vbt, n_q=s, q_off=0, tq=128, tk=512, unroll=2, name="attn_b")

    w_out0 = w_out_attn[0].astype(BF16)
    g0, b0 = ln_g[0][None, :], ln_b[0][None, :]
    x1, h1 = _out0(ya, yb, sg, 0, x, w_out0, lat(gate, 0), g0, b0, lat(shift, 1), lat(scale, 1), tm=512)

    yca = attn_a(qa, ka[:, s:], vat[:, :, s:], n_q=lc, q_off=s, tq=lc, tk=lc, unroll=1,
                 name="attn_a_ctx")
    ycb = attn_b(qb, kb[:, s:], vbt[:, :, s:], n_q=lc, q_off=s, tq=lc // 2, tk=lc, unroll=1,
                 name="attn_b_ctx")
    g_ctx = jnp.broadcast_to(gate[0, b], (b, 1, d))
    zeros = jnp.zeros((b, 1, d), F32)
    xc1, _ = _out0(yca, ycb, sg, s, ctx, w_out0, g_ctx, g0, b0, zeros, zeros, tm=lc)
    del xc1

    h1 = h1.reshape(b * s, d)
    w1 = w_in_fourier[0].astype(BF16)
    u = _matmul(h1, w1[:, :C_WIDTH], gate=False, tm=2048, tn=1024, name="proj1_u")
    sg1 = _matmul(h1, w1[:, C_WIDTH:], gate=True, tm=2048, tn=1024, name="proj1_gate")
    d1, m2, cc, sc = _dft_matrices()
    z = _position_dft(u.reshape(b, s, C_WIDTH), d1, m2)
    return _out1(z, cc, sc, sg1.reshape(b, s, C_WIDTH), x1, w_out_fourier[0].astype(BF16),
                 lat(gate, 1), ln_g[1][None, :], ln_b[1][None, :], tm=256)
```

```python
import functools
import math

import numpy as np
import jax
import jax.numpy as jnp
from jax import lax
from jax.experimental import pallas as pl
from jax.experimental.pallas import tpu as pltpu

F32 = jnp.float32
BF16 = jnp.bfloat16

D_MODEL = 2048
DEPTH = 2
GRID_W = 64
ROPE_THETA = 10000.0
LN_EPS = 1e-6
RMS_EPS = 1e-6

A_HEADS = 8
A_Q_LORA = 768
A_KV_LORA = 512
A_NOPE = 128
A_ROPE = 64
A_V = 128
A_SCALE = (A_NOPE + A_ROPE) ** -0.5
B_HEADS = 8
B_KV_HEADS = 2
B_GROUP = B_HEADS // B_KV_HEADS
B_HEAD_DIM = 128
B_SCALE = B_HEAD_DIM ** -0.5
A_WIDTH = A_HEADS * A_V
B_WIDTH = B_HEADS * B_HEAD_DIM
ATTN_WIDTH = A_WIDTH + B_WIDTH
B_KV_WIDTH = B_KV_HEADS * B_HEAD_DIM

C_EXPAND = 2
C_WIDTH = C_EXPAND * D_MODEL
C_GROUPS = 16
C_GROUP_DIM = C_WIDTH // C_GROUPS

DEEPNORM_ALPHA = (2.0 * DEPTH) ** 0.25

LANE = 128
PACK = 16
A_QK = A_NOPE + LANE
VMEM_LIMIT = 56 * 1024 * 1024

OFF_CQ = 0
OFF_CKV = OFF_CQ + A_Q_LORA
OFF_KPE = OFF_CKV + A_KV_LORA
OFF_QB = OFF_KPE + LANE
OFF_KB = OFF_QB + B_WIDTH
OFF_GATE = OFF_KB + B_KV_WIDTH
PROJ0_WIDTH = OFF_GATE + ATTN_WIDTH

LOG2E = math.log2(math.e)
CONTRACT_LAST = (((1,), (1,)), ((), ()))

FFT_N1 = 128
FFT_N2 = 64


def _silu(x):
    return x / (1.0 + jnp.exp(-x))


def _layer_norm(x):
    mu = jnp.mean(x, axis=-1, keepdims=True)
    xc = x - mu
    var = jnp.mean(xc * xc, axis=-1, keepdims=True)
    return xc * lax.rsqrt(var + LN_EPS)


def _rms(x):
    return x * lax.rsqrt(jnp.mean(x * x, axis=-1, keepdims=True) + RMS_EPS)


def _rope(x, cos, sin_signed):
    return x * cos + pltpu.roll(x, LANE // 2, 1) * sin_signed


def _params(*sem):
    return pltpu.CompilerParams(dimension_semantics=sem, vmem_limit_bytes=VMEM_LIMIT)


def _resident(shape):
    return pl.BlockSpec(shape, lambda *_: (0,) * len(shape), pipeline_mode=pl.Buffered(1))


def _mod_kernel(cv_ref, w_ref, b_ref, o_ref):
    s = _silu(cv_ref[...])
    o_ref[0] = jnp.dot(s, w_ref[0], preferred_element_type=F32,
                       precision=lax.Precision.HIGHEST) + b_ref[0]


def _modulation(cvecs, ada_w, ada_b):
    rows = cvecs.shape[0]
    tn = 768
    return pl.pallas_call(
        _mod_kernel,
        grid=(DEPTH, 3 * D_MODEL // tn),
        in_specs=[
            pl.BlockSpec((rows, D_MODEL), lambda i, j: (0, 0)),
            pl.BlockSpec((1, D_MODEL, tn), lambda i, j: (i, 0, j)),
            pl.BlockSpec((1, 1, tn), lambda i, j: (i, 0, j)),
        ],
        out_specs=pl.BlockSpec((1, rows, tn), lambda i, j: (i, 0, j)),
        out_shape=jax.ShapeDtypeStruct((DEPTH, rows, 3 * D_MODEL), F32),
        compiler_params=_params("arbitrary", "arbitrary"),
        name="modulation",
    )(cvecs, ada_w, ada_b.reshape(DEPTH, 1, 3 * D_MODEL))


def _proj0_kernel(x_ref, ctx_ref, sh_ref, sc_ref, ca_ref, sa_ref, cb_ref, sb_ref,
                  w_ref, wvbt_ref, gq_ref, gkv_ref, wq_ref, wk_ref, wvt_ref, qn_ref, kn_ref,
                  qa_ref, ka_ref, vat_ref, qb_ref, kb_ref, vbt_ref, sg_ref, *, n_lat):
    is_ctx = pl.program_id(1) >= n_lat
    xt = jnp.where(is_ctx, ctx_ref[0], x_ref[0])
    h = (_layer_norm(xt) * (1.0 + sc_ref[0, 0]) + sh_ref[0, 0]).astype(BF16)

    def proj(lo, hi):
        return jnp.dot(h, w_ref[:, lo:hi], preferred_element_type=F32)

    ca, sa, cb, sb = ca_ref[...], sa_ref[...], cb_ref[...], sb_ref[...]

    cq = (_rms(proj(OFF_CQ, OFF_CKV)) * gq_ref[...]).astype(BF16)
    qa = jnp.dot(cq, wq_ref[...], preferred_element_type=F32)
    for hd in range(A_HEADS):
        lo = hd * A_QK
        qa_ref[0, :, lo:lo + A_NOPE] = (qa[:, lo:lo + A_NOPE] * (A_SCALE * LOG2E)).astype(BF16)
        qa_ref[0, :, lo + A_NOPE:lo + A_QK] = (
            _rope(qa[:, lo + A_NOPE:lo + A_QK], ca, sa) * (A_SCALE * LOG2E)).astype(BF16)

    ckv = (_rms(proj(OFF_CKV, OFF_KPE)) * gkv_ref[...]).astype(BF16)
    kn = jnp.dot(ckv, wk_ref[...], preferred_element_type=F32)
    kpe = _rope(proj(OFF_KPE, OFF_QB), ca, sa).astype(BF16)
    for hd in range(A_HEADS):
        lo = hd * A_QK
        ka_ref[0, :, lo:lo + A_NOPE] = kn[:, hd * A_NOPE:(hd + 1) * A_NOPE].astype(BF16)
        ka_ref[0, :, lo + A_NOPE:lo + A_QK] = kpe
    vat_ref[0] = lax.dot_general(wvt_ref[...], ckv, CONTRACT_LAST,
                                 preferred_element_type=F32).astype(BF16)

    qb = proj(OFF_QB, OFF_KB)
    for hd in range(B_HEADS):
        lo = hd * B_HEAD_DIM
        n = _rms(qb[:, lo:lo + B_HEAD_DIM]) * qn_ref[...]
        qb_ref[0, :, lo:lo + B_HEAD_DIM] = (_rope(n, cb, sb) * (B_SCALE * LOG2E)).astype(BF16)
    kb = proj(OFF_KB, OFF_GATE)
    for hd in range(B_KV_HEADS):
        lo = hd * B_HEAD_DIM
        n = _rms(kb[:, lo:lo + B_HEAD_DIM]) * kn_ref[...]
        kb_ref[0, :, lo:lo + B_HEAD_DIM] = _rope(n, cb, sb).astype(BF16)
    vbt_ref[0] = lax.dot_general(wvbt_ref[...], h, CONTRACT_LAST,
                                 preferred_element_type=F32).astype(BF16)

    sg_ref[0] = _silu(proj(OFF_GATE, PROJ0_WIDTH)).astype(BF16)


def _proj0(x, ctx, sh, sc, tabs, w, wvbt, gq, gkv, wq, wk, wvt, qn, kn):
    b, s, d = x.shape
    lc = ctx.shape[1]
    tm = lc
    assert s % tm == 0
    n_lat = s // tm
    lk = s + lc

    def tok(width):
        return (pl.BlockSpec((1, tm, width), lambda bi, i: (bi, i, 0)),
                jax.ShapeDtypeStruct((b, lk, width), BF16))

    def chan(width):
        return (pl.BlockSpec((1, width, tm), lambda bi, i: (bi, 0, i)),
                jax.ShapeDtypeStruct((b, width, lk), BF16))

    outs = [tok(A_HEADS * A_QK), tok(A_HEADS * A_QK), chan(A_WIDTH), tok(B_WIDTH),
            tok(B_KV_WIDTH), chan(B_KV_WIDTH), tok(ATTN_WIDTH)]
    tab = pl.BlockSpec((tm, LANE), lambda bi, i: (i, 0))
    mod = pl.BlockSpec((1, 1, 1, d), lambda bi, i: (bi, jnp.where(i >= n_lat, 1, 0), 0, 0))
    weights = (w, wvbt, gq, gkv, wq, wk, wvt, qn, kn)
    return pl.pallas_call(
        functools.partial(_proj0_kernel, n_lat=n_lat),
        grid=(b, n_lat + 1),
        in_specs=[
            pl.BlockSpec((1, tm, d), lambda bi, i: (bi, jnp.minimum(i, n_lat - 1), 0)),
            pl.BlockSpec((1, tm, d), lambda bi, i: (bi, 0, 0)),
            mod, mod, tab, tab, tab, tab,
        ] + [_resident(a.shape) for a in weights],
        out_specs=[o[0] for o in outs],
        out_shape=[o[1] for o in outs],
        compiler_params=_params("arbitrary", "arbitrary"),
        name="proj0",
    )(x, ctx, sh, sc, *tabs, *weights)


def _flash_kernel(q_ref, k_ref, vt_ref, o_ref, acc_ref, *, n_rep, dk, tk, unroll):
    tq = q_ref.shape[1]
    dv = vt_ref.shape[1]
    lk = k_ref.shape[1]
    q = q_ref[0]
    if n_rep > 1:
        q = jnp.concatenate([q[:, r * dk:(r + 1) * dk] for r in range(n_rep)], axis=0)
    rows = n_rep * tq
    acc_ref[...] = jnp.zeros(acc_ref.shape, F32)

    def step(k0, size, m, l):
        k = k_ref[0, pl.ds(k0, size), :]
        vt = vt_ref[0, :, pl.ds(k0, size)]
        st = lax.dot_general(k, q, CONTRACT_LAST, preferred_element_type=F32)
        m_new = jnp.maximum(m, jnp.max(st, axis=0, keepdims=True))
        p = jnp.exp2(st - m_new)
        alpha = jnp.exp2(m - m_new)
        l = alpha * l + jnp.sum(p, axis=0, keepdims=True)
        acc_ref[...] = alpha * acc_ref[...] + jnp.dot(vt, p.astype(BF16), preferred_element_type=F32)
        return m_new, l

    m = jnp.full((1, rows), -jnp.inf, F32)
    l = jnp.zeros((1, rows), F32)
    n_groups = lk // (tk * unroll)
    if n_groups > 0:
        def body(g, carry):
            m, l = carry
            for u in range(unroll):
                m, l = step(pl.multiple_of((g * unroll + u) * tk, tk), tk, m, l)
            return m, l
        m, l = lax.fori_loop(0, n_groups, body, (m, l))
    k0 = n_groups * tk * unroll
    while k0 < lk:
        size = min(tk, lk - k0)
        m, l = step(k0, size, m, l)
        k0 += size

    out = (acc_ref[...] / l).T
    for r in range(n_rep):
        o_ref[0, :, r * dv:(r + 1) * dv] = out[r * tq:(r + 1) * tq].astype(o_ref.dtype)


def _attention(q, k, vt, *, n_kv, n_rep, dk, dv, n_q, q_off, tq, tk, unroll, name):
    b, lk, _ = k.shape
    assert n_q % tq == 0 and q_off % tq == 0
    tk = min(tk, lk)
    return pl.pallas_call(
        functools.partial(_flash_kernel, n_rep=n_rep, dk=dk, tk=tk, unroll=unroll),
        grid=(b, n_kv, n_q // tq),
        in_specs=[
            pl.BlockSpec((1, tq, n_rep * dk), lambda bi, g, i: (bi, i + q_off // tq, g)),
            pl.BlockSpec((1, lk, dk), lambda bi, g, i: (bi, 0, g)),
            pl.BlockSpec((1, dv, lk), lambda bi, g, i: (bi, g, 0)),
        ],
        out_specs=pl.BlockSpec((1, tq, n_rep * dv), lambda bi, g, i: (bi, i, g)),
        out_shape=jax.ShapeDtypeStruct((b, n_q, n_kv * n_rep * dv), BF16),
        scratch_shapes=[pltpu.VMEM((dv, n_rep * tq), F32)],
        compiler_params=_params("arbitrary", "arbitrary", "arbitrary"),
        name=name,
    )(q, k, vt)


def _post_norm(x, out, gate, g, bias):
    return _layer_norm(DEEPNORM_ALPHA * x + gate * out) * g + bias


def _out0_kernel(ya_ref, yb_ref, sg_ref, x_ref, w_ref, gate_ref, g_ref, b_ref, sh_ref, sc_ref,
                 x1_ref, h1_ref):
    sg = sg_ref[0].astype(F32)
    ga = (ya_ref[0].astype(F32) * sg[:, :A_WIDTH]).astype(BF16)
    gb = (yb_ref[0].astype(F32) * sg[:, A_WIDTH:]).astype(BF16)
    out = (jnp.dot(ga, w_ref[:A_WIDTH, :], preferred_element_type=F32)
           + jnp.dot(gb, w_ref[A_WIDTH:, :], preferred_element_type=F32))
    x1 = _post_norm(x_ref[0], out, gate_ref[0], g_ref[...], b_ref[...])
    x1_ref[0] = x1
    h1_ref[0] = (_layer_norm(x1) * (1.0 + sc_ref[0]) + sh_ref[0]).astype(BF16)


def _out0(ya, yb, sg, sg_off, x, w, gate, g, bias, sh, sc, *, tm):
    b, s, d = x.shape
    assert s % tm == 0 and sg_off % tm == 0

    def tok(width, off=0):
        return pl.BlockSpec((1, tm, width), lambda bi, i: (bi, i + off // tm, 0))

    vec = pl.BlockSpec((1, 1, d), lambda bi, i: (bi, 0, 0))
    return pl.pallas_call(
        _out0_kernel,
        grid=(b, s // tm),
        in_specs=[tok(A_WIDTH), tok(B_WIDTH), tok(ATTN_WIDTH, sg_off), tok(d),
                  _resident(w.shape), vec, _resident(g.shape), _resident(bias.shape), vec, vec],
        out_specs=[tok(d), tok(d)],
        out_shape=[jax.ShapeDtypeStruct((b, s, d), F32), jax.ShapeDtypeStruct((b, s, d), BF16)],
        compiler_params=_params("arbitrary", "arbitrary"),
        name="out0",
    )(ya, yb, sg, x, w, gate, g, bias, sh, sc)


def _matmul_kernel(x_ref, w_ref, o_ref, *, gate):
    z = jnp.dot(x_ref[...], w_ref[...], preferred_element_type=F32)
    o_ref[...] = (_silu(z) if gate else z).astype(o_ref.dtype)


def _matmul(x, w, *, gate, tm, tn, name):
    m, kdim = x.shape
    n = w.shape[1]
    assert m % tm == 0 and n % tn == 0
    return pl.pallas_call(
        functools.partial(_matmul_kernel, gate=gate),
        grid=(m // tm, n // tn),
        in_specs=[pl.BlockSpec((tm, kdim), lambda i, j: (i, 0)),
                  pl.BlockSpec((kdim, tn), lambda i, j: (0, j))],
        out_specs=pl.BlockSpec((tm, tn), lambda i, j: (i, j)),
        out_shape=jax.ShapeDtypeStruct((m, n), BF16),
        compiler_params=_params("arbitrary", "arbitrary"),
        name=name,
    )(x, w)


def _lane_block_scratch(rows, cols):
    return pltpu.VMEM((cols // LANE, rows, LANE), F32)


def _store_lane_blocks(ref, row_idx, value):
    for cb in range(ref.shape[0]):
        ref[cb, row_idx, :] = value[:, cb * LANE:(cb + 1) * LANE]


def _load_lane_blocks(ref, row_idx):
    return jnp.concatenate([ref[cb, row_idx, :] for cb in range(ref.shape[0])], axis=-1)


def _proj1_dft1_kernel(h_ref, w_ref, d_ref, o_ref, u_ref, y_ref):
    rows = FFT_N2 * PACK
    _store_lane_blocks(u_ref, slice(None), jnp.dot(
        h_ref[0].reshape(rows, h_ref.shape[-1]), w_ref[...], preferred_element_type=F32))
    for j in range(PACK):
        x = _load_lane_blocks(u_ref, pl.ds(j, FFT_N2, stride=PACK)).astype(BF16)
        _store_lane_blocks(y_ref, pl.ds(j, 2 * FFT_N2, stride=PACK),
                           jnp.dot(d_ref[...], x, preferred_element_type=F32))
    y = _load_lane_blocks(y_ref, slice(None))
    o_ref[0] = y.reshape(2, FFT_N2, PACK, y.shape[-1]).astype(BF16)


def _dft2_kernel(y_ref, m_ref, o_ref, z_ref):
    for kk in range(PACK):
        y = jnp.concatenate([y_ref[0, 0, kk], y_ref[0, 1, kk]], axis=0)
        _store_lane_blocks(z_ref, pl.ds(kk, 2 * FFT_N1, stride=PACK),
                           jnp.dot(m_ref[kk], y, preferred_element_type=F32))
    z = _load_lane_blocks(z_ref, slice(None))
    o_ref[0] = z.reshape(2, FFT_N1, PACK, z.shape[-1]).astype(BF16)


def _proj1_position_dft(h1, w, d1, m2, *, tn, tc):
    b, l, d = h1.shape
    c = w.shape[1]
    assert l == FFT_N1 * FFT_N2 and c % tn == 0 and c % tc == 0
    y = pl.pallas_call(
        _proj1_dft1_kernel,
        grid=(b, FFT_N1 // PACK, c // tn),
        in_specs=[pl.BlockSpec((1, FFT_N2, PACK, d), lambda bi, i, j: (bi, 0, i, 0)),
                  pl.BlockSpec((d, tn), lambda bi, i, j: (0, j)),
                  _resident(d1.shape)],
        out_specs=pl.BlockSpec((1, 2, FFT_N2, PACK, tn), lambda bi, i, j: (bi, 0, 0, i, j)),
        out_shape=jax.ShapeDtypeStruct((b, 2, FFT_N2, FFT_N1, c), BF16),
        scratch_shapes=[_lane_block_scratch(FFT_N2 * PACK, tn),
                        _lane_block_scratch(2 * FFT_N2 * PACK, tn)],
        compiler_params=_params("arbitrary", "arbitrary", "arbitrary"),
        name="proj1_dft1",
    )(h1.reshape(b, FFT_N2, FFT_N1, d), w, d1)
    z = pl.pallas_call(
        _dft2_kernel,
        grid=(b, FFT_N2 // PACK, c // tc),
        in_specs=[pl.BlockSpec((1, 2, PACK, FFT_N1, tc), lambda bi, i, j: (bi, 0, i, 0, j)),
                  pl.BlockSpec((PACK, 2 * FFT_N1, 2 * FFT_N1), lambda bi, i, j: (i, 0, 0))],
        out_specs=pl.BlockSpec((1, 2, FFT_N1, PACK, tc), lambda bi, i, j: (bi, 0, 0, i, j)),
        out_shape=jax.ShapeDtypeStruct((b, 2, FFT_N1, FFT_N2, c), BF16),
        scratch_shapes=[_lane_block_scratch(2 * FFT_N1 * PACK, tc)],
        compiler_params=_params("arbitrary", "arbitrary", "arbitrary"),
        name="dft_stage2",
    )(y, m2)
    return z.reshape(b, 2, l, c)


def _out1_kernel(zr_ref, zi_ref, cc_ref, sc_ref, sg_ref, x_ref, w_ref, gate_ref, g_ref, b_ref,
                 o_ref, y_ref):
    for grp in range(C_GROUPS):
        lo = grp * C_GROUP_DIM
        f = (jnp.dot(zr_ref[0, 0, :, lo:lo + C_GROUP_DIM], cc_ref[...], preferred_element_type=F32)
             + jnp.dot(zi_ref[0, 0, :, lo:lo + C_GROUP_DIM], sc_ref[...], preferred_element_type=F32))
        y_ref[:, lo:lo + C_GROUP_DIM] = (
            f * sg_ref[0, :, lo:lo + C_GROUP_DIM].astype(F32)).astype(BF16)
    out = jnp.dot(y_ref[...], w_ref[...], preferred_element_type=F32)
    o_ref[0] = _post_norm(x_ref[0], out, gate_ref[0], g_ref[...], b_ref[...])


def _out1(z, cc, sc, sg, x, w, gate, g, bias, *, tm):
    b, s, d = x.shape
    c = z.shape[-1]
    assert s % tm == 0

    def tok(width):
        return pl.BlockSpec((1, tm, width), lambda bi, i: (bi, i, 0))

    vec = pl.BlockSpec((1, 1, d), lambda bi, i: (bi, 0, 0))
    return pl.pallas_call(
        _out1_kernel,
        grid=(b, s // tm),
        in_specs=[pl.BlockSpec((1, 1, tm, c), lambda bi, i: (bi, 0, i, 0)),
                  pl.BlockSpec((1, 1, tm, c), lambda bi, i: (bi, 1, i, 0)),
                  _resident(cc.shape), _resident(sc.shape), tok(c), tok(d),
                  _resident(w.shape), vec, _resident(g.shape), _resident(bias.shape)],
        out_specs=tok(d),
        out_shape=jax.ShapeDtypeStruct((b, s, d), F32),
        scratch_shapes=[pltpu.VMEM((tm, c), BF16)],
        compiler_params=_params("arbitrary", "arbitrary"),
        name="out1",
    )(z, z, cc, sc, sg, x, w, gate, g, bias)


def _split_pairs(w, heads, dim):
    lead = w.shape[:-1]
    w = w.reshape(*lead, heads, dim // 2, 2)
    return jnp.swapaxes(w, -1, -2).reshape(*lead, heads * dim)


def _pe_block(w):
    z = jnp.zeros(w.shape[:-1] + (LANE // 4,), w.dtype)
    return jnp.concatenate([w[..., 0::2], z, w[..., 1::2], z], axis=-1)


def _layout_w_in(w):
    offs = np.cumsum([0, A_Q_LORA, A_KV_LORA, A_ROPE, B_WIDTH, B_KV_WIDTH, B_KV_WIDTH, ATTN_WIDTH])
    cq, ckv, kpe, qb, kb, vb, gate = [w[:, offs[i]:offs[i + 1]] for i in range(7)]
    cols = jnp.concatenate(
        [cq, ckv, _pe_block(kpe), _split_pairs(qb, B_HEADS, B_HEAD_DIM),
         _split_pairs(kb, B_KV_HEADS, B_HEAD_DIM), gate], axis=1)
    return cols.astype(BF16), vb.T.astype(BF16)


def _layout_wq_b(w):
    w = w.reshape(A_Q_LORA, A_HEADS, A_NOPE + A_ROPE)
    w = jnp.concatenate([w[..., :A_NOPE], _pe_block(w[..., A_NOPE:])], axis=-1)
    return w.reshape(A_Q_LORA, A_HEADS * A_QK).astype(BF16)


def _layout_wkv_b(w):
    w = w.reshape(A_KV_LORA, A_HEADS, A_NOPE + A_V)
    wk = w[..., :A_NOPE].reshape(A_KV_LORA, A_HEADS * A_NOPE)
    wv = w[..., A_NOPE:].reshape(A_KV_LORA, A_HEADS * A_V)
    return wk.astype(BF16), wv.T.astype(BF16)


def _rope_tables(rows, n_ctx):
    r = jnp.repeat(jnp.arange(rows, dtype=F32), GRID_W)
    col = jnp.tile(jnp.arange(GRID_W, dtype=F32), rows)

    def angles(rot_dim):
        n_freq = rot_dim // 4
        inv = ROPE_THETA ** (-jnp.arange(n_freq, dtype=F32) / n_freq)
        ang = jnp.concatenate([r[:, None] * inv, col[:, None] * inv], axis=-1)
        return jnp.concatenate([ang, jnp.zeros((n_ctx, rot_dim // 2), F32)], axis=0)

    ang_a = angles(A_ROPE)
    ang_b = angles(B_HEAD_DIM)
    za = jnp.zeros_like(ang_a)
    ca = jnp.concatenate([jnp.cos(ang_a), za, jnp.cos(ang_a), za], axis=-1)
    sa = jnp.concatenate([-jnp.sin(ang_a), za, jnp.sin(ang_a), za], axis=-1)
    cb = jnp.concatenate([jnp.cos(ang_b), jnp.cos(ang_b)], axis=-1)
    sb = jnp.concatenate([-jnp.sin(ang_b), jnp.sin(ang_b)], axis=-1)
    return ca, sa, cb, sb


def _dft_matrices():
    l = FFT_N1 * FFT_N2
    t1 = 2.0 * math.pi * ((jnp.arange(FFT_N2)[:, None] * jnp.arange(FFT_N2)[None, :]) % FFT_N2) / FFT_N2
    d1 = jnp.concatenate([jnp.cos(t1), -jnp.sin(t1)], axis=0) / math.sqrt(FFT_N2)
    k2 = jnp.arange(FFT_N2)[:, None, None]
    k1 = jnp.arange(FFT_N1)[None, :, None]
    n1 = jnp.arange(FFT_N1)[None, None, :]
    t2 = 2.0 * math.pi * ((k1 * n1 * FFT_N2 + n1 * k2) % l).astype(F32) / l
    mr = jnp.cos(t2) / math.sqrt(FFT_N1)
    mi = -jnp.sin(t2) / math.sqrt(FFT_N1)
    m2 = jnp.concatenate([jnp.concatenate([mr, -mi], axis=2),
                          jnp.concatenate([mi, mr], axis=2)], axis=1)
    t3 = 2.0 * math.pi * ((jnp.arange(C_GROUP_DIM)[:, None] * jnp.arange(C_GROUP_DIM)[None, :])
                          % C_GROUP_DIM) / C_GROUP_DIM
    cc = jnp.cos(t3) / math.sqrt(C_GROUP_DIM)
    sc = jnp.sin(t3) / math.sqrt(C_GROUP_DIM)
    return d1.astype(BF16), m2.astype(BF16), cc.astype(BF16), sc.astype(BF16)


def kernel(x, c, ctx, c_ctx, ada_w, ada_b, ln_g, ln_b, w_in_attn, wq_b, q_lora_norm, kv_lora_norm,
           wkv_b, q_norm_b, k_norm_b, w_out_attn, w_in_fourier, w_out_fourier):
    b, s, d = x.shape
    lc = ctx.shape[1]
    assert d == D_MODEL and s % GRID_W == 0

    cvecs = jnp.zeros((8, d), F32).at[:b].set(c).at[b].set(c_ctx)
    mods = _modulation(cvecs, ada_w, ada_b)
    shift, scale, gate = mods[..., :d], mods[..., d:2 * d], mods[..., 2 * d:]

    def lat(m, i):
        return m[i, :b, None, :]

    def both(m, i):
        return jnp.stack([m[i, :b], jnp.broadcast_to(m[i, b], (b, d))], axis=1)[:, :, None, :]

    w0, wvbt = _layout_w_in(w_in_attn[0])
    wk, wvt = _layout_wkv_b(wkv_b[0])
    qa, ka, vat, qb, kb, vbt, sg = _proj0(
        x, ctx, both(shift, 0), both(scale, 0), _rope_tables(s // GRID_W, lc),
        w0, wvbt, q_lora_norm[0][None, :], kv_lora_norm[0][None, :], _layout_wq_b(wq_b[0]), wk, wvt,
        _split_pairs(q_norm_b[0], 1, B_HEAD_DIM)[None, :], _split_pairs(k_norm_b[0], 1, B_HEAD_DIM)[None, :])

    attn_a = functools.partial(_attention, n_kv=A_HEADS, n_rep=1, dk=A_QK, dv=A_V)
    attn_b = functools.partial(_attention, n_kv=B_KV_HEADS, n_rep=B_GROUP, dk=B_HEAD_DIM, dv=B_HEAD_DIM)
    ya = attn_a(qa, ka, vat, n_q=s, q_off=0, tq=2048, tk=512, unroll=2, name="attn_a")
    yb = attn_b(qb, kb, vbt, n_q=s, q_off=0, tq=512, tk=512, unroll=2, name="attn_b")

    w_out0 = w_out_attn[0].astype(BF16)
    g0, b0 = ln_g[0][None, :], ln_b[0][None, :]
    x1, h1 = _out0(ya, yb, sg, 0, x, w_out0, lat(gate, 0), g0, b0, lat(shift, 1), lat(scale, 1), tm=512)

    yca = attn_a(qa, ka[:, s:], vat[:, :, s:], n_q=lc, q_off=s, tq=lc, tk=lc, unroll=1,
                 name="attn_a_ctx")
    ycb = attn_b(qb, kb[:, s:], vbt[:, :, s:], n_q=lc, q_off=s, tq=lc // 2, tk=lc, unroll=1,
                 name="attn_b_ctx")
    g_ctx = jnp.broadcast_to(gate[0, b], (b, 1, d))
    zeros = jnp.zeros((b, 1, d), F32)
    xc1, _ = _out0(yca, ycb, sg, s, ctx, w_out0, g_ctx, g0, b0, zeros, zeros, tm=lc)
    del xc1

    w1 = w_in_fourier[0].astype(BF16)
    d1, m2, cc, sc = _dft_matrices()
    z = _proj1_position_dft(h1, w1[:, :C_WIDTH], d1, m2, tn=1024, tc=512)
    sg1 = _matmul(h1.reshape(b * s, d), w1[:, C_WIDTH:], gate=True, tm=2048, tn=1024, name="proj1_gate")
    return _out1(z, cc, sc, sg1.reshape(b, s, C_WIDTH), x1, w_out_fourier[0].astype(BF16),
                 lat(gate, 1), ln_g[1][None, :], ln_b[1][None, :], tm=256)
```
